```python
import math
import jax
import jax.numpy as jnp
from jax import lax
import numpy as np

D_MODEL = 1024
BATCH = 8
SEQ = 2048
DEPTH = 1

HEAD_DIM = D_MODEL // 16
N_FOX_HEADS = 8
N_MOBA_HEADS = 8
FOX_WIDTH = N_FOX_HEADS * HEAD_DIM
MOBA_WIDTH = N_MOBA_HEADS * HEAD_DIM
IN_COLS = 3 * FOX_WIDTH + N_FOX_HEADS + 3 * MOBA_WIDTH + 2 * D_MODEL
Q_BLOCK = 128
MOBA_BLOCK = 256
MOBA_TOPK = 3
MOBA_Q_CHUNK = 32
PEER_HEADS = 8
PEER_N_KEYS = 128
PEER_N_EXPERTS = PEER_N_KEYS * PEER_N_KEYS
PEER_TOPK = 16
PEER_QUERY_DIM = 256
PEER_HALF = PEER_QUERY_DIM // 2
PEER_TOKEN_CHUNK = 128
EPS = 1e-6
NEG = -1e30

kernel_name = "hybrid_fox_moba_peer_adaln_block"


def rms_norm(x, g):
    xf = x.astype(jnp.float32)
    y = xf * lax.rsqrt(jnp.mean(xf * xf, axis=-1, keepdims=True) + EPS)
    return (y * g.astype(jnp.float32)).astype(x.dtype)


def fox_attention(q, k, v, log_f):
    B, H, S, d = q.shape
    F = jnp.cumsum(log_f, axis=-1)
    nq = S // Q_BLOCK
    qb = q.reshape(B, H, nq, Q_BLOCK, d).transpose(2, 0, 1, 3, 4)
    Fq = F.reshape(B, H, nq, Q_BLOCK).transpose(2, 0, 1, 3)
    kpos = jnp.arange(S)
    scale = d ** -0.5

    def block(args):
        qi, Fi, i = args
        s = jnp.einsum('bhqd,bhkd->bhqk', qi, k).astype(jnp.float32) * scale
        s = s + Fi[..., None] - F[:, :, None, :]
        qpos = i * Q_BLOCK + jnp.arange(Q_BLOCK)
        s = jnp.where(kpos[None, :] <= qpos[:, None], s, NEG)
        p = jax.nn.softmax(s, axis=-1).astype(v.dtype)
        return jnp.einsum('bhqk,bhkd->bhqd', p, v)

    out = lax.map(block, (qb, Fq, jnp.arange(nq)))
    return out.transpose(1, 2, 0, 3, 4).reshape(B, H, S, d)


def moba_attention(q, k, v, slopes):
    B, H, S, d = q.shape
    nb = -(-S // MOBA_BLOCK)
    s_pad = nb * MOBA_BLOCK
    pad = ((0, 0), (0, 0), (0, s_pad - S), (0, 0))
    qp, kp, vp = jnp.pad(q, pad), jnp.pad(k, pad), jnp.pad(v, pad)
    kb = kp.reshape(B, H, nb, MOBA_BLOCK, d)
    vb = vp.reshape(B, H, nb, MOBA_BLOCK, d)
    k_mean = jnp.mean(kb.astype(jnp.float32), axis=3)
    n_sel = min(MOBA_TOPK, nb - 1)
    C = MOBA_Q_CHUNK
    nc = s_pad // C
    qc = qp.reshape(B, H, nc, C, d).transpose(2, 0, 1, 3, 4)
    scale = d ** -0.5
    offs = jnp.arange(MOBA_BLOCK)
    b_idx = jnp.arange(B)[:, None, None, None]
    h_idx = jnp.arange(H)[None, :, None, None]
    slope4 = slopes.astype(jnp.float32)[None, :, None, None]
    slope5 = slope4[..., None]

    def chunk(args):
        qi, ci = args
        t0 = ci * C
        bt = t0 // MOBA_BLOCK
        qpos = t0 + jnp.arange(C)
        k_own = lax.dynamic_index_in_dim(kb, bt, axis=2, keepdims=False)
        v_own = lax.dynamic_index_in_dim(vb, bt, axis=2, keepdims=False)
        own_pos = bt * MOBA_BLOCK + offs
        dist_own = (qpos[:, None] - own_pos[None, :]).astype(jnp.float32)
        s_own = jnp.einsum('bhqd,bhkd->bhqk', qi, k_own).astype(jnp.float32) * scale - slope4 * dist_own
        s_own = jnp.where(dist_own >= 0, s_own, NEG)
        if n_sel == 0:
            p = jax.nn.softmax(s_own, axis=-1).astype(v.dtype)
            return jnp.einsum('bhqk,bhkd->bhqd', p, v_own)
        gate = jnp.einsum('bhqd,bhnd->bhqn', qi.astype(jnp.float32), k_mean)
        gate = jnp.where(jnp.arange(nb) < bt, gate, NEG)
        _, sel = lax.top_k(gate, n_sel)
        k_g = kb[b_idx, h_idx, sel]
        v_g = vb[b_idx, h_idx, sel]
        sel_pos = sel[..., None] * MOBA_BLOCK + offs
        dist_sel = (qpos[None, None, :, None, None] - sel_pos).astype(jnp.float32)
        s_sel = jnp.einsum('bhqd,bhqnkd->bhqnk', qi, k_g).astype(jnp.float32) * scale - slope5 * dist_sel
        s_sel = jnp.where((sel < bt)[..., None], s_sel, NEG)
        logits = jnp.concatenate([s_sel.reshape(B, H, C, n_sel * MOBA_BLOCK), s_own], axis=-1)
        p = jax.nn.softmax(logits, axis=-1).astype(v.dtype)
        p_sel = p[..., :n_sel * MOBA_BLOCK].reshape(B, H, C, n_sel, MOBA_BLOCK)
        p_own = p[..., n_sel * MOBA_BLOCK:]
        return (jnp.einsum('bhqnk,bhqnkd->bhqd', p_sel, v_g)
                + jnp.einsum('bhqk,bhkd->bhqd', p_own, v_own))

    out = lax.map(chunk, (qc, jnp.arange(nc)))
    return out.transpose(1, 2, 0, 3, 4).reshape(B, H, s_pad, d)[:, :, :S]


def peer_ffn(h, w_q, sub_keys, u_tab, v_tab):
    B, S, D = h.shape
    T = B * S
    ht = h.reshape(T, D)
    q = (ht @ w_q).reshape(T, PEER_HEADS, 2, PEER_HALF)
    sc = jnp.einsum('thpc,hpnc->thpn', q, sub_keys).astype(jnp.float32)
    s1, i1 = lax.top_k(sc[:, :, 0], PEER_TOPK)
    s2, i2 = lax.top_k(sc[:, :, 1], PEER_TOPK)
    cand = (s1[..., :, None] + s2[..., None, :]).reshape(T, PEER_HEADS, PEER_TOPK * PEER_TOPK)
    cand_id = (i1[..., :, None] * PEER_N_KEYS + i2[..., None, :]).reshape(T, PEER_HEADS, PEER_TOPK * PEER_TOPK)
    top_s, pos = lax.top_k(cand, PEER_TOPK)
    eid = jnp.take_along_axis(cand_id, pos, axis=-1)
    g = jax.nn.softmax(top_s, axis=-1).astype(h.dtype)
    nc = T // PEER_TOKEN_CHUNK

    def chunk(args):
        hc, ec, gc = args
        a = jnp.einsum('cd,chkd->chk', hc, u_tab[ec])
        w = gc * jax.nn.gelu(a)
        return jnp.einsum('chk,chkd->cd', w, v_tab[ec])

    y = lax.map(chunk, (ht.reshape(nc, PEER_TOKEN_CHUNK, D),
                        eid.reshape(nc, PEER_TOKEN_CHUNK, PEER_HEADS, PEER_TOPK),
                        g.reshape(nc, PEER_TOKEN_CHUNK, PEER_HEADS, PEER_TOPK)))
    return y.reshape(B, S, D)


def setup_inputs(seed: int = 0) -> dict:
    key = jax.random.key(seed)
    ks = jax.random.split(key, 20)
    D = D_MODEL
    nrm = lambda k, shape, s: jax.random.normal(k, shape, jnp.float32) * s
    return {
        "x": nrm(ks[0], (BATCH, SEQ, D), 1.0),
        "c": nrm(ks[1], (BATCH, D), 1.0),
        "w_ada": nrm(ks[2], (DEPTH, D, 6 * D), D ** -0.5),
        "b_ada": nrm(ks[3], (DEPTH, 6 * D), 0.02),
        "norm1_g": 1.0 + nrm(ks[4], (DEPTH, D), 0.05),
        "w_in": nrm(ks[5], (DEPTH, D, IN_COLS), D ** -0.5),
        "b_fgate": jnp.linspace(1.0, 4.0, N_FOX_HEADS, dtype=jnp.float32)[None, :] + nrm(ks[6], (DEPTH, N_FOX_HEADS), 0.1),
        "fox_q_g": 1.0 + nrm(ks[7], (DEPTH, HEAD_DIM), 0.05),
        "fox_k_g": 1.0 + nrm(ks[8], (DEPTH, HEAD_DIM), 0.05),
        "moba_q_g": 1.0 + nrm(ks[9], (DEPTH, HEAD_DIM), 0.05),
        "moba_k_g": 1.0 + nrm(ks[10], (DEPTH, HEAD_DIM), 0.05),
        "w_up_fox": nrm(ks[11], (DEPTH, FOX_WIDTH, D), FOX_WIDTH ** -0.5),
        "w_up_moba": nrm(ks[12], (DEPTH, MOBA_WIDTH, D), MOBA_WIDTH ** -0.5),
        "w_out": nrm(ks[13], (DEPTH, D, D), D ** -0.5),
        "norm2_g": 1.0 + nrm(ks[14], (DEPTH, D), 0.05),
        "w_peer_q": nrm(ks[15], (DEPTH, D, PEER_HEADS * PEER_QUERY_DIM), D ** -0.5),
        "peer_sub_keys": nrm(ks[16], (DEPTH, PEER_HEADS, 2, PEER_N_KEYS, PEER_HALF), PEER_HALF ** -0.5),
        "peer_u": nrm(ks[17], (DEPTH, PEER_N_EXPERTS, D), D ** -0.5),
        "peer_v": nrm(ks[18], (DEPTH, PEER_N_EXPERTS, D), (PEER_HEADS * PEER_TOPK) ** -0.5),
    }


def reference(x, c, w_ada, b_ada, norm1_g, w_in, b_fgate, fox_q_g, fox_k_g, moba_q_g, moba_k_g,
              w_up_fox, w_up_moba, w_out, norm2_g, w_peer_q, peer_sub_keys, peer_u, peer_v):
    B, S, D = x.shape
    slopes = 2.0 ** (-8.0 * jnp.arange(1, N_MOBA_HEADS + 1, dtype=jnp.float32) / N_MOBA_HEADS)
    split_at = list(np.cumsum([FOX_WIDTH, FOX_WIDTH, FOX_WIDTH, N_FOX_HEADS,
                               MOBA_WIDTH, MOBA_WIDTH, MOBA_WIDTH, D_MODEL]))

    def heads(t, n_h):
        return t.reshape(B, S, n_h, HEAD_DIM).transpose(0, 2, 1, 3)

    def merge_heads(t):
        return t.transpose(0, 2, 1, 3).reshape(B, S, -1)

    silu_c = jax.nn.silu(c)
    for l in range(DEPTH):
        mod = (silu_c @ w_ada[l] + b_ada[l])[:, None, :]
        shift1, scale1, gate1, shift2, scale2, gate2 = jnp.split(mod, 6, axis=-1)

        h = rms_norm(x, norm1_g[l]) * (1.0 + scale1) + shift1
        z = h @ w_in[l]
        q_a, k_a, v_a, f_a, q_b, k_b, v_b, g_a, g_b = jnp.split(z, split_at, axis=-1)

        q_a = rms_norm(heads(q_a, N_FOX_HEADS), fox_q_g[l])
        k_a = rms_norm(heads(k_a, N_FOX_HEADS), fox_k_g[l])
        log_f = jax.nn.log_sigmoid((f_a + b_fgate[l]).astype(jnp.float32)).transpose(0, 2, 1)
        y_a = merge_heads(fox_attention(q_a, k_a, heads(v_a, N_FOX_HEADS), log_f))

        q_b = rms_norm(heads(q_b, N_MOBA_HEADS), moba_q_g[l])
        k_b = rms_norm(heads(k_b, N_MOBA_HEADS), moba_k_g[l])
        y_b = merge_heads(moba_attention(q_b, k_b, heads(v_b, N_MOBA_HEADS), slopes))

        mixed = jax.nn.sigmoid(g_a) * (y_a @ w_up_fox[l]) + jax.nn.sigmoid(g_b) * (y_b @ w_up_moba[l])
        x = x + gate1 * (mixed @ w_out[l])

        h2 = rms_norm(x, norm2_g[l]) * (1.0 + scale2) + shift2
        x = x + gate2 * peer_ffn(h2, w_peer_q[l], peer_sub_keys[l], peer_u[l], peer_v[l])
    return x
```

```python
import functools

import jax
import jax.numpy as jnp
from jax import lax
from jax.experimental import pallas as pl
from jax.experimental.pallas import tpu as pltpu

F32 = jnp.float32
BF16 = jnp.bfloat16
HIGHEST = lax.Precision.HIGHEST

HEAD_DIM = 64
N_HEADS = 8
BRANCH_WIDTH = N_HEADS * HEAD_DIM
MOBA_BLOCK = 256
MOBA_TOPK = 3
PEER_HEADS = 8
PEER_KEYS = 128
PEER_TOPK = 16
PEER_HALF = 128
EPS = 1e-6
NEG = -1e30
LANES = 128
VMEM_LIMIT = 56 * 1024 * 1024

NT_DIMS = (((1,), (1,)), ((), ()))


def _log_sigmoid(x):
    return jnp.minimum(x, 0.0) - jnp.log(1.0 + jnp.exp(-jnp.abs(x)))


def _sigmoid(x):
    return 1.0 / (1.0 + jnp.exp(-x))


def _gelu_tanh(x):
    c = 0.7978845608028654
    return 0.5 * x * (1.0 + jnp.tanh(c * (x + 0.044715 * (x * x * x))))


def _ada_kernel(c_ref, w_ref, b_ref, o_ref):
    c = c_ref[...]
    sc = c * _sigmoid(c)
    o_ref[...] = jnp.dot(sc, w_ref[...], preferred_element_type=F32, precision=HIGHEST) + b_ref[...]


def _ada_mod(c, w_ada, b_ada, tn=512):
    bsz, d = c.shape
    n = w_ada.shape[1]
    return pl.pallas_call(
        _ada_kernel,
        grid=(n // tn,),
        in_specs=[pl.BlockSpec((bsz, d), lambda j: (0, 0)),
                  pl.BlockSpec((d, tn), lambda j: (0, j)),
                  pl.BlockSpec((1, tn), lambda j: (0, j))],
        out_specs=pl.BlockSpec((bsz, tn), lambda j: (0, j)),
        out_shape=jax.ShapeDtypeStruct((bsz, n), F32),
        compiler_params=pltpu.CompilerParams(dimension_semantics=("parallel",)),
        name="ada_mod",
    )(c, w_ada, b_ada.reshape(1, n))


def _in_proj_kernel(x_ref, mod_ref, g1_ref, wm_ref, wf_ref, wft_ref, bf_ref, bft_ref, gains_ref, bd_ref,
                    qa_ref, ka_ref, va_ref, qb_ref, kb_ref, vb_ref, sga_ref, sgb_ref,
                    fcol_ref, frow_ref, kmean_ref, ccol_ref, crow_ref):
    i = pl.program_id(1)
    tt = x_ref.shape[1]
    w = BRANCH_WIDTH
    x = x_ref[0]
    ms = jnp.mean(x * x, axis=-1, keepdims=True)
    shift = mod_ref[0, 0:1, :]
    scale = mod_ref[0, 1:2, :]
    h = (x * lax.rsqrt(ms + EPS) * g1_ref[...]) * (1.0 + scale) + shift
    hb = h.astype(BF16)

    def proj(g):
        return jnp.dot(hb, wm_ref[:, g * w:(g + 1) * w], preferred_element_type=F32)

    def head_norm(z, row):
        ss = jnp.dot((z * z).astype(BF16), bd_ref[...], preferred_element_type=F32)
        return z * lax.rsqrt(ss * (1.0 / HEAD_DIM) + EPS) * gains_ref[row:row + 1, :]

    qa_ref[...] = head_norm(proj(0), 0).astype(BF16)
    ka_ref[...] = head_norm(proj(1), 1).astype(BF16)
    va_ref[...] = proj(2).astype(BF16)
    qb_ref[...] = head_norm(proj(3), 2).astype(BF16)
    kb = head_norm(proj(4), 3)
    kb_ref[...] = kb.astype(BF16)
    nsub = tt // MOBA_BLOCK
    for s in range(nsub):
        km = jnp.mean(kb[s * MOBA_BLOCK:(s + 1) * MOBA_BLOCK], axis=0, keepdims=True)
        kmean_ref[0, pl.ds(i * nsub + s, 1), :] = km
    vb_ref[...] = proj(5).astype(BF16)
    for g in range(2):
        sga_ref[:, g * w:(g + 1) * w] = _sigmoid(proj(6 + g)).astype(BF16)
        sgb_ref[:, g * w:(g + 1) * w] = _sigmoid(proj(8 + g)).astype(BF16)

    @pl.when(i == 0)
    def _():
        ccol_ref[...] = jnp.zeros_like(ccol_ref)
        crow_ref[...] = jnp.zeros_like(crow_ref)

    r_io = lax.broadcasted_iota(jnp.int32, (tt, tt), 0)
    c_io = lax.broadcasted_iota(jnp.int32, (tt, tt), 1)
    fc = jnp.dot(hb, wf_ref[...], preferred_element_type=F32)[:, :N_HEADS] + bf_ref[...]
    lfc = _log_sigmoid(fc)
    ltri = (c_io <= r_io).astype(F32)
    fcum = jnp.dot(ltri, lfc, preferred_element_type=F32, precision=HIGHEST) + ccol_ref[...]
    fcol_ref[...] = fcum
    ccol_ref[...] = fcum[tt - 1:tt, :]
    fr = lax.dot_general(wft_ref[...], hb, NT_DIMS, preferred_element_type=F32) + bft_ref[...]
    lfr = _log_sigmoid(fr)
    utri = (r_io <= c_io).astype(F32)
    frcum = jnp.dot(lfr, utri, preferred_element_type=F32, precision=HIGHEST) + crow_ref[...]
    for sb in range(tt // MOBA_BLOCK):
        frow_ref[0, sb] = frcum[:, sb * MOBA_BLOCK:(sb + 1) * MOBA_BLOCK]
    crow_ref[...] = frcum[:, tt - 1:tt]


def _in_proj(x, mod, g1, wmain, wf, wft, bfg, gains, bd, tt=512):
    bsz, s, d = x.shape
    t = bsz * s
    ns = s // tt
    nb = s // MOBA_BLOCK
    w = BRANCH_WIDTH
    ncols = wmain.shape[1]
    const = lambda b, i: (0, 0)
    tok = lambda b, i: (b * ns + i, 0)
    act = jax.ShapeDtypeStruct((t, w), BF16)
    gate = jax.ShapeDtypeStruct((t, 2 * w), BF16)
    return pl.pallas_call(
        _in_proj_kernel,
        grid=(bsz, ns),
        in_specs=[pl.BlockSpec((1, tt, d), lambda b, i: (b, i, 0)),
                  pl.BlockSpec((1, 6, d), lambda b, i: (b, 0, 0)),
                  pl.BlockSpec((1, d), const),
                  pl.BlockSpec((d, ncols), const),
                  pl.BlockSpec((d, LANES), const),
                  pl.BlockSpec((N_HEADS, d), const),
                  pl.BlockSpec((1, N_HEADS), const),
                  pl.BlockSpec((N_HEADS, 1), const),
                  pl.BlockSpec((4, w), const),
                  pl.BlockSpec((w, w), const)],
        out_specs=[pl.BlockSpec((tt, w), tok)] * 6 + [pl.BlockSpec((tt, 2 * w), tok)] * 2 + [
            pl.BlockSpec((tt, N_HEADS), tok),
            pl.BlockSpec((1, tt // MOBA_BLOCK, N_HEADS, MOBA_BLOCK), lambda b, i: (b, i, 0, 0)),
            pl.BlockSpec((1, nb, w), lambda b, i: (b, 0, 0))],
        out_shape=[act] * 6 + [gate] * 2 + [
            jax.ShapeDtypeStruct((t, N_HEADS), F32),
            jax.ShapeDtypeStruct((bsz, nb, N_HEADS, MOBA_BLOCK), F32),
            jax.ShapeDtypeStruct((bsz, nb, w), F32)],
        scratch_shapes=[pltpu.VMEM((1, N_HEADS), F32), pltpu.VMEM((N_HEADS, 1), F32)],
        compiler_params=pltpu.CompilerParams(dimension_semantics=("parallel", "arbitrary"),
                                             vmem_limit_bytes=VMEM_LIMIT),
        name="in_proj",
    )(x, mod, g1, wmain, wf, wft, bfg.reshape(1, N_HEADS), bfg.reshape(N_HEADS, 1), gains, bd)


def _attn_kernel(*refs, mode, tq):
    if mode == "fox":
        q_ref, k_ref, v_ref, fcol_ref, frow_ref, o_ref = refs
    else:
        q_ref, k_ref, v_ref, kmean_ref, o_ref = refs
    i = pl.program_id(1)
    scale = HEAD_DIM ** -0.5
    lane = lax.broadcasted_iota(jnp.int32, (1, LANES), 1)
    r_io = lax.broadcasted_iota(jnp.int32, (tq, tq), 0)
    c_io = lax.broadcasted_iota(jnp.int32, (tq, tq), 1)
    causal = c_io <= r_io
    if mode == "moba":
        base = (r_io - c_io).astype(F32)
        nb = kmean_ref.shape[1]
        blk = lax.broadcasted_iota(jnp.int32, (1, nb), 1)

    for pair in range(N_HEADS // 2):
        lsl = slice(pair * LANES, (pair + 1) * LANES)
        q2 = q_ref[:, lsl]
        outs = []
        for hh in range(2):
            head = 2 * pair + hh
            inhalf = (lane >= HEAD_DIM * hh) & (lane < HEAD_DIM * (hh + 1))
            qm = jnp.where(inhalf, q2, jnp.zeros_like(q2))

            if mode == "fox":
                col = fcol_ref[:, head:head + 1]
            else:
                slope = 2.0 ** (-8.0 * (head + 1) / N_HEADS)
                gate = lax.dot_general(qm.astype(F32), kmean_ref[0, :, lsl], NT_DIMS,
                                       preferred_element_type=F32, precision=HIGHEST)
                gv = jnp.where(blk < i, gate, NEG)
                cnt = jnp.zeros(gv.shape, F32)
                for m in range(nb):
                    gm = gv[:, m:m + 1]
                    ahead = (gm > gv) | ((gm == gv) & (m < blk))
                    cnt = cnt + ahead.astype(F32)
                chosen = (cnt < float(MOBA_TOPK)) & (blk < i)
                bias = jnp.where(chosen, 0.0, NEG)

            def logits(n):
                kblk = k_ref[pl.ds(pl.multiple_of(n * tq, tq), tq), lsl]
                s = lax.dot_general(qm, kblk, NT_DIMS, preferred_element_type=F32) * scale
                if mode == "fox":
                    return s + col - frow_ref[0, n, pl.ds(head, 1), :]
                dist = base + ((i - n) * tq).astype(F32)
                return s - slope * dist

            def pv(p, n):
                vblk = v_ref[pl.ds(pl.multiple_of(n * tq, tq), tq), lsl]
                return jnp.dot(p.astype(BF16), vblk, preferred_element_type=F32)

            s = jnp.where(causal, logits(i), NEG)
            m0 = jnp.max(s, axis=-1, keepdims=True)
            p = jnp.exp(s - m0)
            l0 = jnp.sum(p, axis=-1, keepdims=True)
            acc0 = pv(p, i)

            def body(n, carry):
                m, l, acc = carry
                s = logits(n)
                if mode == "moba":
                    s = s + jnp.sum(jnp.where(blk == n, bias, 0.0), axis=-1, keepdims=True)
                m_new = jnp.maximum(m, jnp.max(s, axis=-1, keepdims=True))
                alpha = jnp.exp(m - m_new)
                p = jnp.exp(s - m_new)
                l = alpha * l + jnp.sum(p, axis=-1, keepdims=True)
                acc = alpha * acc + pv(p, n)
                return m_new, l, acc

            m, l, acc = lax.fori_loop(0, i, body, (m0, l0, acc0))
            outs.append(acc / l)
        o_ref[:, lsl] = jnp.where(lane < HEAD_DIM, outs[0], outs[1]).astype(BF16)


def _attention(mode, q, k, v, extras, bsz, s, tq=MOBA_BLOCK):
    t, w = q.shape
    nq = s // tq
    in_specs = [pl.BlockSpec((tq, w), lambda b, i: (b * nq + i, 0)),
                pl.BlockSpec((s, w), lambda b, i: (b, 0)),
                pl.BlockSpec((s, w), lambda b, i: (b, 0))]
    if mode == "fox":
        fcol, frow = extras
        in_specs += [pl.BlockSpec((tq, N_HEADS), lambda b, i: (b * nq + i, 0)),
                     pl.BlockSpec((1, nq, N_HEADS, tq), lambda b, i: (b, 0, 0, 0))]
    else:
        (kmean,) = extras
        nb = kmean.shape[1]
        in_specs += [pl.BlockSpec((1, nb, w), lambda b, i: (b, 0, 0))]
    return pl.pallas_call(
        functools.partial(_attn_kernel, mode=mode, tq=tq),
        grid=(bsz, nq),
        in_specs=in_specs,
        out_specs=pl.BlockSpec((tq, w), lambda b, i: (b * nq + i, 0)),
        out_shape=jax.ShapeDtypeStruct((t, w), BF16),
        compiler_params=pltpu.CompilerParams(dimension_semantics=("parallel", "arbitrary"),
                                             vmem_limit_bytes=VMEM_LIMIT),
        name=f"attn_{mode}",
    )(q, k, v, *extras)


def _merge_kernel(ya_ref, yb_ref, sga_ref, sgb_ref, x_ref, mod_ref, wuf_ref, wum_ref, wout_ref, g2_ref, wqt_ref,
                  x1_ref, h2_ref, qt_ref):
    ua = jnp.dot(ya_ref[...], wuf_ref[...], preferred_element_type=F32)
    ub = jnp.dot(yb_ref[...], wum_ref[...], preferred_element_type=F32)
    mixed = sga_ref[...].astype(F32) * ua + sgb_ref[...].astype(F32) * ub
    d = jnp.dot(mixed.astype(BF16), wout_ref[...], preferred_element_type=F32)
    x1 = x_ref[...] + mod_ref[0, 2:3, :] * d
    x1_ref[...] = x1
    ms = jnp.mean(x1 * x1, axis=-1, keepdims=True)
    h2 = (x1 * lax.rsqrt(ms + EPS) * g2_ref[...]) * (1.0 + mod_ref[0, 4:5, :]) + mod_ref[0, 3:4, :]
    h2b = h2.astype(BF16)
    h2_ref[...] = h2b
    qt_ref[...] = lax.dot_general(wqt_ref[...], h2b, NT_DIMS, preferred_element_type=F32)


def _merge(ya, yb, sga, sgb, x2, mod, wuf, wum, wout, g2, wqt, s, tt=512):
    t, d = x2.shape
    w = BRANCH_WIDTH
    nq = wqt.shape[0]
    per = s // tt
    const = lambda i: (0, 0)
    tok = lambda i: (i, 0)
    return pl.pallas_call(
        _merge_kernel,
        grid=(t // tt,),
        in_specs=[pl.BlockSpec((tt, w), tok), pl.BlockSpec((tt, w), tok),
                  pl.BlockSpec((tt, d), tok), pl.BlockSpec((tt, d), tok),
                  pl.BlockSpec((tt, d), tok),
                  pl.BlockSpec((1, 6, d), lambda i: (i // per, 0, 0)),
                  pl.BlockSpec((w, d), const), pl.BlockSpec((w, d), const),
                  pl.BlockSpec((d, d), const), pl.BlockSpec((1, d), const),
                  pl.BlockSpec((nq, d), const)],
        out_specs=[pl.BlockSpec((tt, d), tok), pl.BlockSpec((tt, d), tok),
                   pl.BlockSpec((nq, tt), lambda i: (0, i))],
        out_shape=[jax.ShapeDtypeStruct((t, d), F32), jax.ShapeDtypeStruct((t, d), BF16),
                   jax.ShapeDtypeStruct((nq, t), F32)],
        compiler_params=pltpu.CompilerParams(dimension_semantics=("parallel",),
                                             vmem_limit_bytes=VMEM_LIMIT),
        name="merge",
    )(ya, yb, sga, sgb, x2, mod, wuf, wum, wout, g2, wqt)


def _top16(sc):
    nk, t = sc.shape
    idx = lax.broadcasted_iota(jnp.int32, (nk, t), 0).astype(F32)
    row = lax.broadcasted_iota(jnp.int32, (PEER_TOPK, t), 0)
    rank = jnp.full((nk, t), float(PEER_TOPK), F32)
    vals = jnp.zeros((PEER_TOPK, t), F32)
    for r in range(PEER_TOPK):
        m = jnp.max(sc, axis=0, keepdims=True)
        first = jnp.min(jnp.where(sc == m, idx, float(nk)), axis=0, keepdims=True)
        hit = idx == first
        sc = jnp.where(hit, -jnp.inf, sc)
        rank = jnp.where(hit, float(r), rank)
        vals = jnp.where(row == r, m, vals)
    return vals, rank


_CAND_GROUPS = (
    (0, 0, 0, 1, 8), (0, 0, 8, 1, 8), (1, 0, 0, 1, 8), (2, 0, 0, 1, 5), (3, 0, 0, 1, 4),
    (4, 0, 0, 1, 3), (5, 0, 0, 1, 2), (6, 0, 0, 1, 2), (7, 0, 0, 1, 2), (8, 1, 0, 0, 8),
)


def _pair_groups(a1, a2, op, fill):
    t = a1.shape[1]
    sub = lax.broadcasted_iota(jnp.int32, (8, t), 0)
    tiles = []
    for r1, r1s, r2, r2s, valid in _CAND_GROUPS:
        x1 = a1[r1:r1 + 8, :] if r1s else a1[r1:r1 + 1, :]
        x2 = a2[r2:r2 + 8, :] if r2s else a2[r2:r2 + 1, :]
        v = op(x1, x2)
        if v.shape[0] != 8:
            v = jnp.broadcast_to(v, (8, t))
        tiles.append(v if valid == 8 else jnp.where(sub < valid, v, fill))
    return tiles


def _peer_topk_kernel(qt_ref, subk_ref, e1_ref, n1_ref, e2_ref, r2_ref):
    t = qt_ref.shape[1]
    sub = lax.broadcasted_iota(jnp.int32, (8, t), 0).astype(F32)
    pos = jnp.concatenate(
        [(r1 + r1s * sub) * float(PEER_TOPK) + (r2 + r2s * sub) for r1, r1s, r2, r2s, _ in _CAND_GROUPS], axis=0)
    ncand = pos.shape[0]

    def head_body(h, carry):
        def scores(p):
            qh = qt_ref[pl.ds(pl.multiple_of((2 * h + p) * PEER_HALF, PEER_HALF), PEER_HALF), :]
            return jnp.dot(subk_ref[2 * h + p], qh, preferred_element_type=F32, precision=HIGHEST)

        s1 = scores(0)
        s2 = scores(1)
        v1, rank1 = _top16(s1)
        v2, rank2 = _top16(s2)

        cand = jnp.concatenate(_pair_groups(v1, v2, lambda a, b: a + b, -jnp.inf), axis=0)
        taken = jnp.zeros((ncand, t), F32)
        for _ in range(PEER_TOPK):
            m = jnp.max(cand, axis=0, keepdims=True)
            first = jnp.min(jnp.where(cand == m, pos, 1e9), axis=0, keepdims=True)
            hit = pos == first
            cand = jnp.where(hit, -jnp.inf, cand)
            taken = jnp.where(hit, 1.0, taken)

        rows = []
        rows.append(jnp.sum(taken[0:16], axis=0, keepdims=True))
        for g in range(2, 9):
            rows.append(jnp.sum(taken[8 * g:8 * g + 8], axis=0, keepdims=True))
        row8 = lax.broadcasted_iota(jnp.int32, (8, t), 0)
        n_lo = jnp.zeros((8, t), F32)
        for k in range(8):
            n_lo = jnp.where(row8 == k, rows[k], n_lo)
        n16 = jnp.concatenate([n_lo, taken[72:80]], axis=0)

        ex1 = jnp.exp(v1 - v1[0:1, :])
        ex2 = jnp.exp(v2 - v2[0:1, :])
        wts = jnp.concatenate(_pair_groups(ex1, ex2, lambda a, b: a * b, 0.0), axis=0)
        z = jnp.sum(taken * wts, axis=0, keepdims=True)
        inv_z = 1.0 / z

        n_dense = jnp.zeros((PEER_KEYS, t), F32)
        for r in range(PEER_TOPK):
            n_dense = jnp.where(rank1 == float(r), n16[r:r + 1, :], n_dense)
        e1_ref[h] = jnp.where(rank1 < float(PEER_TOPK), jnp.exp(s1 - v1[0:1, :]) * inv_z, 0.0)
        n1_ref[h] = n_dense
        e2_ref[h] = jnp.exp(s2 - v2[0:1, :])
        r2_ref[h] = rank2
        return carry

    lax.fori_loop(0, PEER_HEADS, head_body, 0)


def _peer_topk(qt, subk, tt=256):
    nq, t = qt.shape
    tab = jax.ShapeDtypeStruct((PEER_HEADS, PEER_KEYS, t), F32)
    tab_spec = pl.BlockSpec((PEER_HEADS, PEER_KEYS, tt), lambda i: (0, 0, i))
    return pl.pallas_call(
        _peer_topk_kernel,
        grid=(t // tt,),
        in_specs=[pl.BlockSpec((nq, tt), lambda i: (0, i)),
                  pl.BlockSpec(subk.shape, lambda i: (0, 0, 0))],
        out_specs=[tab_spec] * 4,
        out_shape=[tab] * 4,
        compiler_params=pltpu.CompilerParams(dimension_semantics=("parallel",),
                                             vmem_limit_bytes=VMEM_LIMIT),
        name="peer_topk",
    )(qt, subk)


def _peer_main_kernel(h2_ref, u_ref, vt_ref, e1_ref, n1_ref, e2_ref, r2_ref, x1_ref, mod_ref,
                      o_ref, acc_ref, wt_ref):
    e = pl.program_id(1)
    te = u_ref.shape[0]
    per = te // PEER_KEYS

    @pl.when(e == 0)
    def _():
        acc_ref[...] = jnp.zeros_like(acc_ref)

    at = lax.dot_general(u_ref[...], h2_ref[...], NT_DIMS, preferred_element_type=F32)
    for ii in range(per):
        i1 = e * per + ii
        g = jnp.zeros((PEER_KEYS, at.shape[1]), F32)
        for h in range(PEER_HEADS):
            nrow = n1_ref[h, pl.ds(i1, 1), :]
            erow = e1_ref[h, pl.ds(i1, 1), :]
            g = g + jnp.where(r2_ref[h] < nrow, e2_ref[h], 0.0) * erow
        a = at[ii * PEER_KEYS:(ii + 1) * PEER_KEYS, :]
        wt_ref[ii * PEER_KEYS:(ii + 1) * PEER_KEYS, :] = (_gelu_tanh(a) * g).astype(BF16)
    acc_ref[...] += jnp.dot(vt_ref[...], wt_ref[...], preferred_element_type=F32)

    @pl.when(e == pl.num_programs(1) - 1)
    def _():
        o_ref[...] = x1_ref[...] + mod_ref[0, 5:6, :] * acc_ref[...].T


def _peer_main(h2, u, vt, e1, n1, e2, r2, x1, mod, s, tt=512, te=512):
    t, d = h2.shape
    ne = u.shape[0]
    per = s // tt
    tab_spec = pl.BlockSpec((PEER_HEADS, PEER_KEYS, tt), lambda i, e: (0, 0, i))
    return pl.pallas_call(
        _peer_main_kernel,
        grid=(t // tt, ne // te),
        in_specs=[pl.BlockSpec((tt, d), lambda i, e: (i, 0)),
                  pl.BlockSpec((te, d), lambda i, e: (e, 0)),
                  pl.BlockSpec((d, te), lambda i, e: (0, e)),
                  tab_spec, tab_spec, tab_spec, tab_spec,
                  pl.BlockSpec((tt, d), lambda i, e: (i, 0)),
                  pl.BlockSpec((1, 6, d), lambda i, e: (i // per, 0, 0))],
        out_specs=pl.BlockSpec((tt, d), lambda i, e: (i, 0)),
        out_shape=jax.ShapeDtypeStruct((t, d), F32),
        scratch_shapes=[pltpu.VMEM((d, tt), F32), pltpu.VMEM((te, tt), BF16)],
        compiler_params=pltpu.CompilerParams(dimension_semantics=("parallel", "arbitrary"),
                                             vmem_limit_bytes=VMEM_LIMIT),
        name="peer_main",
    )(h2, u, vt, e1, n1, e2, r2, x1, mod)


def _layer(x, silu_in, w_ada, b_ada, norm1_g, w_in, b_fgate, fox_q_g, fox_k_g, moba_q_g, moba_k_g,
           w_up_fox, w_up_moba, w_out, norm2_g, w_peer_q, peer_sub_keys, peer_u, peer_v):
    bsz, s, d = x.shape
    w = BRANCH_WIDTH
    mod = _ada_mod(silu_in, w_ada, b_ada).reshape(bsz, 6, d)

    c0 = 3 * w
    c1 = c0 + N_HEADS
    wmain = jnp.concatenate([w_in[:, :c0], w_in[:, c1:]], axis=1).astype(BF16)
    wf_cols = w_in[:, c0:c1]
    wf = jnp.pad(wf_cols, ((0, 0), (0, LANES - N_HEADS))).astype(BF16)
    wft = wf_cols.T.astype(BF16)
    gains = jnp.stack([jnp.tile(g, N_HEADS) for g in (fox_q_g, fox_k_g, moba_q_g, moba_k_g)])
    hid = jnp.arange(w) // HEAD_DIM
    bd = (hid[:, None] == hid[None, :]).astype(BF16)

    qa, ka, va, qb, kb, vb, sga, sgb, fcol, frow, kmean = _in_proj(
        x, mod, norm1_g.reshape(1, d), wmain, wf, wft, b_fgate, gains, bd)
    ya = _attention("fox", qa, ka, va, (fcol, frow), bsz, s)
    yb = _attention("moba", qb, kb, vb, (kmean,), bsz, s)

    x1, h2, qt = _merge(ya, yb, sga, sgb, x.reshape(bsz * s, d), mod,
                        w_up_fox.astype(BF16), w_up_moba.astype(BF16), w_out.astype(BF16),
                        norm2_g.reshape(1, d), w_peer_q.T.astype(BF16), s)
    subk = peer_sub_keys.reshape(2 * PEER_HEADS, PEER_KEYS, PEER_HALF)
    e1, n1, e2, r2 = _peer_topk(qt, subk)
    out = _peer_main(h2, peer_u.astype(BF16), peer_v.T.astype(BF16), e1, n1, e2, r2, x1, mod, s)
    return out.reshape(bsz, s, d)


def kernel(x, c, w_ada, b_ada, norm1_g, w_in, b_fgate, fox_q_g, fox_k_g, moba_q_g, moba_k_g,
           w_up_fox, w_up_moba, w_out, norm2_g, w_peer_q, peer_sub_keys, peer_u, peer_v):
    depth = w_ada.shape[0]
    for l in range(depth):
        x = _layer(x, c, w_ada[l], b_ada[l], norm1_g[l], w_in[l], b_fgate[l], fox_q_g[l], fox_k_g[l],
                   moba_q_g[l], moba_k_g[l], w_up_fox[l], w_up_moba[l], w_out[l], norm2_g[l],
                   w_peer_q[l], peer_sub_keys[l], peer_u[l], peer_v[l])
    return x
```

```python
import functools

import numpy as np
import jax
import jax.numpy as jnp
from jax import lax
from jax.experimental import pallas as pl
from jax.experimental.pallas import tpu as pltpu

F32 = jnp.float32
BF16 = jnp.bfloat16
HIGHEST = lax.Precision.HIGHEST

HEAD_DIM = 64
N_HEADS = 8
BRANCH_WIDTH = N_HEADS * HEAD_DIM
MOBA_BLOCK = 256
MOBA_TOPK = 3
PEER_HEADS = 8
PEER_KEYS = 128
PEER_TOPK = 16
PEER_HALF = 128
EPS = 1e-6
NEG = -1e30
LANES = 128
HEAD_LANES = LANES
VMEM_LIMIT = 56 * 1024 * 1024

NT_DIMS = (((1,), (1,)), ((), ()))

FOX_ONE_ROW = 3 * N_HEADS
MOBA_P1_ROW = N_HEADS * N_HEADS
MOBA_KHOT_ROW = 8


def _log_sigmoid(x):
    return jnp.minimum(x, 0.0) - jnp.log(1.0 + jnp.exp(-jnp.abs(x)))


def _sigmoid(x):
    return 1.0 / (1.0 + jnp.exp(-x))


def _gelu_tanh(x):
    c = 0.7978845608028654
    hx = 0.5 * x
    return hx + hx * jnp.tanh(x * (c + (c * 0.044715) * (x * x)))


def _aux_base(head):
    return head * HEAD_LANES + (HEAD_DIM if head % 2 == 0 else 0)


def _placement_matrices():
    width = N_HEADS * HEAD_LANES
    fox = np.zeros((LANES, 2 * width), np.float32)
    mq = np.zeros((LANES, width), np.float32)
    mk = np.zeros((LANES, width), np.float32)
    for h in range(N_HEADS):
        ab = _aux_base(h)
        slope = 2.0 ** (-8.0 * (h + 1) / N_HEADS)
        for piece in range(3):
            fox[piece * N_HEADS + h, ab + piece] = 1.0
            fox[FOX_ONE_ROW, ab + 3 + piece] = 1.0
            fox[FOX_ONE_ROW, width + ab + piece] = 1.0
            fox[piece * N_HEADS + h, width + ab + 3 + piece] = -1.0
        mq[MOBA_P1_ROW, ab + 0] = -slope
        mq[MOBA_P1_ROW + 1, ab + 1] = -slope
        mq[MOBA_P1_ROW + 2, ab + 2] = 1.0
        mq[MOBA_P1_ROW + 2, ab + 3] = 1.0
        mk[2, ab + 0] = 1.0
        mk[2, ab + 1] = 1.0
        mk[0, ab + 2] = slope
        mk[1, ab + 3] = slope
        for n in range(N_HEADS):
            mq[h * N_HEADS + n, ab + 8 + n] = 1.0
            mk[MOBA_KHOT_ROW + n, ab + 8 + n] = 1.0
    return fox, mq, mk


def _ada_kernel(c_ref, w_ref, b_ref, o_ref):
    c = c_ref[...]
    sc = c * _sigmoid(c)
    o_ref[...] = jnp.dot(sc, w_ref[...], preferred_element_type=F32, precision=HIGHEST) + b_ref[...]


def _ada_mod(c, w_ada, b_ada, tn=512):
    bsz, d = c.shape
    n = w_ada.shape[1]
    return pl.pallas_call(
        _ada_kernel,
        grid=(n // tn,),
        in_specs=[pl.BlockSpec((bsz, d), lambda j: (0, 0)),
                  pl.BlockSpec((d, tn), lambda j: (0, j)),
                  pl.BlockSpec((1, tn), lambda j: (0, j))],
        out_specs=pl.BlockSpec((bsz, tn), lambda j: (0, j)),
        out_shape=jax.ShapeDtypeStruct((bsz, n), F32),
        compiler_params=pltpu.CompilerParams(dimension_semantics=("parallel",)),
        name="ada_mod",
    )(c, w_ada, b_ada.reshape(1, n))


def _in_proj_kernel(x_ref, mod_ref, g1_ref, wm_ref, wvt_ref, wf_ref, bf_ref, gains_ref, bd_ref,
                    efox_ref, emq_ref, emk_ref,
                    qxa_ref, kxa_ref, vta_ref, qxb_ref, kxb_ref, vtb_ref, sga_ref, sgb_ref,
                    carry_ref, kmean_ref):
    i = pl.program_id(1)
    tt = x_ref.shape[1]
    w = BRANCH_WIDTH
    width = N_HEADS * HEAD_LANES
    nsub = tt // MOBA_BLOCK
    x = x_ref[0]
    ms = jnp.mean(x * x, axis=-1, keepdims=True)
    shift = mod_ref[0, 0:1, :]
    scale = mod_ref[0, 1:2, :]
    h = (x * lax.rsqrt(ms + EPS) * g1_ref[...]) * (1.0 + scale) + shift
    hb = h.astype(BF16)

    def proj(g):
        return jnp.dot(hb, wm_ref[:, g * w:(g + 1) * w], preferred_element_type=F32)

    def head_norm(z, row):
        ss = jnp.dot((z * z).astype(BF16), bd_ref[...], preferred_element_type=F32)
        return z * lax.rsqrt(ss * (1.0 / HEAD_DIM) + EPS) * gains_ref[row:row + 1, :]

    lane = lax.broadcasted_iota(jnp.int32, (1, LANES), 1)
    own_half = (lane < HEAD_DIM, lane >= HEAD_DIM)

    def expand(z):
        return jnp.concatenate(
            [jnp.where(own_half[hd % 2], z[:, (hd // 2) * LANES:(hd // 2 + 1) * LANES], 0.0)
             for hd in range(N_HEADS)], axis=1)

    def value_t(g):
        return lax.dot_general(wvt_ref[g], hb, NT_DIMS, preferred_element_type=F32).astype(BF16)

    def store_vt(ref, vt):
        for sb in range(nsub):
            ref[0, sb] = vt[:, sb * MOBA_BLOCK:(sb + 1) * MOBA_BLOCK]

    @pl.when(i == 0)
    def _():
        carry_ref[...] = jnp.zeros_like(carry_ref)
        kmean_ref[...] = jnp.zeros_like(kmean_ref)

    r_io = lax.broadcasted_iota(jnp.int32, (tt, tt), 0)
    c_io = lax.broadcasted_iota(jnp.int32, (tt, tt), 1)
    fc = jnp.dot(hb, wf_ref[...], preferred_element_type=F32) + bf_ref[...]
    ltri = (c_io <= r_io).astype(F32)
    fcum = jnp.dot(ltri, _log_sigmoid(fc), preferred_element_type=F32, precision=HIGHEST) + carry_ref[...]
    carry_ref[...] = fcum[tt - 1:tt, :]
    hi = fcum.astype(BF16).astype(F32)
    rem = fcum - hi
    mid = rem.astype(BF16).astype(F32)
    lo = rem - mid
    src = jnp.where(lane < N_HEADS, hi, jnp.where(lane < 2 * N_HEADS, mid, jnp.where(lane < FOX_ONE_ROW, lo, 0.0)))
    src = jnp.where(lane == FOX_ONE_ROW, 1.0, src).astype(BF16)
    aux = jnp.dot(src, efox_ref[...], preferred_element_type=F32)
    qscale = HEAD_DIM ** -0.5
    qxa_ref[...] = (expand(head_norm(proj(0), 0) * qscale) + aux[:, :width]).astype(BF16)
    kxa_ref[...] = (expand(head_norm(proj(1), 1)) + aux[:, width:]).astype(BF16)
    store_vt(vta_ref, value_t(0))

    qb = head_norm(proj(2), 2)
    kb = head_norm(proj(3), 3)
    for sb in range(nsub):
        kmean_ref[pl.ds(i * nsub + sb, 1), :] = jnp.mean(kb[sb * MOBA_BLOCK:(sb + 1) * MOBA_BLOCK],
                                                         axis=0, keepdims=True)
    km = kmean_ref[...]
    lane_head = lax.broadcasted_iota(jnp.int32, (1, w), 1) // HEAD_DIM
    kmt = jnp.concatenate([jnp.where(lane_head == hd, km, 0.0) for hd in range(N_HEADS)], axis=0)
    gate_t = lax.dot_general(kmt, qb, NT_DIMS, preferred_element_type=F32, precision=HIGHEST)
    pos_row = i * tt + lax.broadcasted_iota(jnp.int32, (1, tt), 1)
    bq_row = pos_row // MOBA_BLOCK
    blk = lax.broadcasted_iota(jnp.int32, (N_HEADS, tt), 0)
    valid = blk < bq_row
    rows = []
    for hd in range(N_HEADS):
        gv = jnp.where(valid, gate_t[hd * N_HEADS:(hd + 1) * N_HEADS], NEG)
        cnt = jnp.zeros(gv.shape, F32)
        for m in range(N_HEADS):
            gm = gv[m:m + 1, :]
            cnt = cnt + ((gm > gv) | ((gm == gv) & (m < blk))).astype(F32)
        chosen = (cnt < float(MOBA_TOPK)) & valid
        rows.append(jnp.where(chosen | (blk == bq_row), 0.0, NEG))
    p1 = (bq_row * MOBA_BLOCK).astype(F32)
    p2 = (pos_row % MOBA_BLOCK).astype(F32)
    rows.append(jnp.where(blk == 0, p1, jnp.where(blk == 1, p2, jnp.where(blk == 2, 1.0, 0.0))))
    rows.append(jnp.zeros((LANES - MOBA_P1_ROW - N_HEADS, tt), F32))
    src_q = jnp.concatenate(rows, axis=0).T.astype(BF16)
    auxq = jnp.dot(src_q, emq_ref[...], preferred_element_type=F32)
    pos_col = i * tt + lax.broadcasted_iota(jnp.int32, (tt, 1), 0)
    bk_col = pos_col // MOBA_BLOCK
    src_k = jnp.where(lane == 0, (bk_col * MOBA_BLOCK).astype(F32),
                      jnp.where(lane == 1, (pos_col % MOBA_BLOCK).astype(F32),
                                jnp.where((lane == 2) | (lane - MOBA_KHOT_ROW == bk_col), 1.0, 0.0)))
    auxk = jnp.dot(src_k.astype(BF16), emk_ref[...], preferred_element_type=F32)
    qxb_ref[...] = (expand(qb * qscale) + auxq).astype(BF16)
    kxb_ref[...] = (expand(kb) + auxk).astype(BF16)
    store_vt(vtb_ref, value_t(1))

    for g in range(2):
        sga_ref[:, g * w:(g + 1) * w] = _sigmoid(proj(4 + g)).astype(BF16)
        sgb_ref[:, g * w:(g + 1) * w] = _sigmoid(proj(6 + g)).astype(BF16)


def _in_proj(x, mod, g1, wmain, wvt, wf, bf, gains, bd, efox, emq, emk, tt=512):
    bsz, s, d = x.shape
    t = bsz * s
    ns = s // tt
    nb = s // MOBA_BLOCK
    nsub = tt // MOBA_BLOCK
    w = BRANCH_WIDTH
    width = N_HEADS * HEAD_LANES
    const = lambda b, i: (0, 0)
    tok = lambda b, i: (b * ns + i, 0)
    qk = jax.ShapeDtypeStruct((t, width), BF16)
    vt = jax.ShapeDtypeStruct((bsz, nb, w, MOBA_BLOCK), BF16)
    gate = jax.ShapeDtypeStruct((t, 2 * w), BF16)
    qk_spec = pl.BlockSpec((tt, width), tok)
    vt_spec = pl.BlockSpec((1, nsub, w, MOBA_BLOCK), lambda b, i: (b, i, 0, 0))
    gate_spec = pl.BlockSpec((tt, 2 * w), tok)
    return pl.pallas_call(
        _in_proj_kernel,
        grid=(bsz, ns),
        in_specs=[pl.BlockSpec((1, tt, d), lambda b, i: (b, i, 0)),
                  pl.BlockSpec((1, 6, d), lambda b, i: (b, 0, 0)),
                  pl.BlockSpec((1, d), const),
                  pl.BlockSpec(wmain.shape, const),
                  pl.BlockSpec(wvt.shape, lambda b, i: (0, 0, 0)),
                  pl.BlockSpec((d, LANES), const),
                  pl.BlockSpec((1, LANES), const),
                  pl.BlockSpec((4, w), const),
                  pl.BlockSpec((w, w), const),
                  pl.BlockSpec(efox.shape, const),
                  pl.BlockSpec(emq.shape, const),
                  pl.BlockSpec(emk.shape, const)],
        out_specs=[qk_spec, qk_spec, vt_spec, qk_spec, qk_spec, vt_spec, gate_spec, gate_spec],
        out_shape=[qk, qk, vt, qk, qk, vt, gate, gate],
        scratch_shapes=[pltpu.VMEM((1, LANES), F32), pltpu.VMEM((nb, w), F32)],
        compiler_params=pltpu.CompilerParams(dimension_semantics=("parallel", "arbitrary"),
                                             vmem_limit_bytes=VMEM_LIMIT),
        name="in_proj",
    )(x, mod, g1, wmain, wvt, wf, bf, gains, bd, efox, emq, emk)


def _attn_kernel(qx_ref, kx_ref, vt_ref, o_ref, acc_ref, m_ref, l_ref):
    i = pl.program_id(1)
    tq = qx_ref.shape[0]
    tk = tq
    key_io = lax.broadcasted_iota(jnp.int32, (tq, tq), 0)
    qry_io = lax.broadcasted_iota(jnp.int32, (tq, tq), 1)
    causal = key_io <= qry_io
    even_rows = lax.broadcasted_iota(jnp.int32, (LANES, 1), 0) < HEAD_DIM

    m_ref[...] = jnp.full(m_ref.shape, NEG, F32)
    l_ref[...] = jnp.zeros_like(l_ref)
    acc_ref[...] = jnp.zeros_like(acc_ref)

    def chunk_step(n, sub, masked):
        keys = pl.ds(pl.multiple_of(n * tq + sub * tk, tk), tk)
        scores = []
        for hd in range(N_HEADS):
            hsl = slice(hd * HEAD_LANES, (hd + 1) * HEAD_LANES)
            scores.append(lax.dot_general(kx_ref[keys, hsl], qx_ref[:, hsl], NT_DIMS,
                                          preferred_element_type=F32))
        m_all = m_ref[...]
        l_all = l_ref[...]
        probs, alphas, m_rows, l_rows = [], [], [], []
        for hd in range(N_HEADS):
            s = scores[hd]
            if masked:
                s = jnp.where(causal[sub * tk:(sub + 1) * tk, :], s, NEG)
            m_old = m_all[hd:hd + 1, :]
            m_new = jnp.maximum(m_old, jnp.max(s, axis=0, keepdims=True))
            alpha = jnp.exp(m_old - m_new)
            p = jnp.exp(s - m_new)
            l_rows.append(alpha * l_all[hd:hd + 1, :] + jnp.sum(p, axis=0, keepdims=True))
            m_rows.append(m_new)
            probs.append(p.astype(BF16))
            alphas.append(alpha)
        m_ref[...] = jnp.concatenate(m_rows, axis=0)
        l_ref[...] = jnp.concatenate(l_rows, axis=0)
        for pair in range(N_HEADS // 2):
            vt = vt_ref[0, n, pair * LANES:(pair + 1) * LANES, sub * tk:(sub + 1) * tk]
            zero = jnp.zeros_like(vt)
            lhs = jnp.concatenate([jnp.where(even_rows, vt, zero), jnp.where(even_rows, zero, vt)], axis=1)
            pv = jnp.dot(lhs, jnp.concatenate(probs[2 * pair:2 * pair + 2], axis=0),
                         preferred_element_type=F32)
            acc_ref[pair] = jnp.where(even_rows, alphas[2 * pair], alphas[2 * pair + 1]) * acc_ref[pair] + pv

    for sub in range(tq // tk):
        chunk_step(i, sub, True)

    def body(n, carry):
        for sub in range(tq // tk):
            chunk_step(n, sub, False)
        return carry

    lax.fori_loop(0, i, body, 0)

    for pair in range(N_HEADS // 2):
        l_pair = jnp.where(even_rows, l_ref[2 * pair:2 * pair + 1, :], l_ref[2 * pair + 1:2 * pair + 2, :])
        o_ref[:, pair * LANES:(pair + 1) * LANES] = (acc_ref[pair] / l_pair).T.astype(BF16)


def _attention(name, qx, kx, vt, bsz, s, tq=MOBA_BLOCK):
    t, width = qx.shape
    w = BRANCH_WIDTH
    nq = s // tq
    return pl.pallas_call(
        _attn_kernel,
        grid=(bsz, nq),
        in_specs=[pl.BlockSpec((tq, width), lambda b, i: (b * nq + i, 0)),
                  pl.BlockSpec((s, width), lambda b, i: (b, 0)),
                  pl.BlockSpec((1, nq, w, tq), lambda b, i: (b, 0, 0, 0))],
        out_specs=pl.BlockSpec((tq, w), lambda b, i: (b * nq + i, 0)),
        out_shape=jax.ShapeDtypeStruct((t, w), BF16),
        scratch_shapes=[pltpu.VMEM((N_HEADS // 2, LANES, tq), F32),
                        pltpu.VMEM((N_HEADS, tq), F32), pltpu.VMEM((N_HEADS, tq), F32)],
        compiler_params=pltpu.CompilerParams(dimension_semantics=("parallel", "arbitrary"),
                                             vmem_limit_bytes=VMEM_LIMIT),
        name=name,
    )(qx, kx, vt)


def _merge_kernel(ya_ref, yb_ref, sga_ref, sgb_ref, x_ref, mod_ref, wuf_ref, wum_ref, wout_ref, g2_ref, wqt_ref,
                  x1_ref, h2_ref, qt_ref):
    ua = jnp.dot(ya_ref[...], wuf_ref[...], preferred_element_type=F32)
    ub = jnp.dot(yb_ref[...], wum_ref[...], preferred_element_type=F32)
    mixed = sga_ref[...].astype(F32) * ua + sgb_ref[...].astype(F32) * ub
    d = jnp.dot(mixed.astype(BF16), wout_ref[...], preferred_element_type=F32)
    x1 = x_ref[...] + mod_ref[0, 2:3, :] * d
    x1_ref[...] = x1
    ms = jnp.mean(x1 * x1, axis=-1, keepdims=True)
    h2 = (x1 * lax.rsqrt(ms + EPS) * g2_ref[...]) * (1.0 + mod_ref[0, 4:5, :]) + mod_ref[0, 3:4, :]
    h2t = h2.T.astype(BF16)
    h2_ref[...] = h2t
    qt_ref[...] = jnp.dot(wqt_ref[...], h2t, preferred_element_type=F32)


def _merge(ya, yb, sga, sgb, x2, mod, wuf, wum, wout, g2, wqt, s, tt=512):
    t, d = x2.shape
    w = BRANCH_WIDTH
    nq = wqt.shape[0]
    per = s // tt
    const = lambda i: (0, 0)
    tok = lambda i: (i, 0)
    return pl.pallas_call(
        _merge_kernel,
        grid=(t // tt,),
        in_specs=[pl.BlockSpec((tt, w), tok), pl.BlockSpec((tt, w), tok),
                  pl.BlockSpec((tt, d), tok), pl.BlockSpec((tt, d), tok),
                  pl.BlockSpec((tt, d), tok),
                  pl.BlockSpec((1, 6, d), lambda i: (i // per, 0, 0)),
                  pl.BlockSpec((w, d), const), pl.BlockSpec((w, d), const),
                  pl.BlockSpec((d, d), const), pl.BlockSpec((1, d), const),
                  pl.BlockSpec((nq, d), const)],
        out_specs=[pl.BlockSpec((tt, d), tok), pl.BlockSpec((d, tt), lambda i: (0, i)),
                   pl.BlockSpec((nq, tt), lambda i: (0, i))],
        out_shape=[jax.ShapeDtypeStruct((t, d), F32), jax.ShapeDtypeStruct((d, t), BF16),
                   jax.ShapeDtypeStruct((nq, t), F32)],
        compiler_params=pltpu.CompilerParams(dimension_semantics=("parallel",),
                                             vmem_limit_bytes=VMEM_LIMIT),
        name="merge",
    )(ya, yb, sga, sgb, x2, mod, wuf, wum, wout, g2, wqt)


def _top16(sc):
    nk, t = sc.shape
    idx = lax.broadcasted_iota(jnp.int32, (nk, t), 0).astype(F32)
    row = lax.broadcasted_iota(jnp.int32, (PEER_TOPK, t), 0)
    rank = jnp.full((nk, t), float(PEER_TOPK), F32)
    vals = jnp.zeros((PEER_TOPK, t), F32)
    for r in range(PEER_TOPK):
        m = jnp.max(sc, axis=0, keepdims=True)
        first = jnp.min(jnp.where(sc == m, idx, float(nk)), axis=0, keepdims=True)
        hit = idx == first
        sc = jnp.where(hit, -jnp.inf, sc)
        rank = jnp.where(hit, float(r), rank)
        vals = jnp.where(row == r, m, vals)
    return vals, rank


_CAND_GROUPS = (
    (0, 0, 0, 1, 8), (0, 0, 8, 1, 8), (1, 0, 0, 1, 8), (2, 0, 0, 1, 5), (3, 0, 0, 1, 4),
    (4, 0, 0, 1, 3), (5, 0, 0, 1, 2), (6, 0, 0, 1, 2), (7, 0, 0, 1, 2), (8, 1, 0, 0, 8),
)


def _pair_groups(a1, a2, op, fill):
    t = a1.shape[1]
    sub = lax.broadcasted_iota(jnp.int32, (8, t), 0)
    tiles = []
    for r1, r1s, r2, r2s, valid in _CAND_GROUPS:
        x1 = a1[r1:r1 + 8, :] if r1s else a1[r1:r1 + 1, :]
        x2 = a2[r2:r2 + 8, :] if r2s else a2[r2:r2 + 1, :]
        v = op(x1, x2)
        if v.shape[0] != 8:
            v = jnp.broadcast_to(v, (8, t))
        tiles.append(v if valid == 8 else jnp.where(sub < valid, v, fill))
    return tiles


def _peer_topk_kernel(qt_ref, subk_ref, e1_ref, n1_ref, e2_ref, r2_ref):
    t = qt_ref.shape[1]
    sub = lax.broadcasted_iota(jnp.int32, (8, t), 0).astype(F32)
    pos = jnp.concatenate(
        [(r1 + r1s * sub) * float(PEER_TOPK) + (r2 + r2s * sub) for r1, r1s, r2, r2s, _ in _CAND_GROUPS], axis=0)
    ncand = pos.shape[0]

    def head_body(h, carry):
        def scores(p):
            qh = qt_ref[pl.ds(pl.multiple_of((2 * h + p) * PEER_HALF, PEER_HALF), PEER_HALF), :]
            return jnp.dot(subk_ref[2 * h + p], qh, preferred_element_type=F32, precision=HIGHEST)

        s1 = scores(0)
        s2 = scores(1)
        v1, rank1 = _top16(s1)
        v2, rank2 = _top16(s2)

        cand = jnp.concatenate(_pair_groups(v1, v2, lambda a, b: a + b, -jnp.inf), axis=0)
        taken = jnp.zeros((ncand, t), F32)
        for _ in range(PEER_TOPK):
            m = jnp.max(cand, axis=0, keepdims=True)
            first = jnp.min(jnp.where(cand == m, pos, 1e9), axis=0, keepdims=True)
            hit = pos == first
            cand = jnp.where(hit, -jnp.inf, cand)
            taken = jnp.where(hit, 1.0, taken)

        rows = []
        rows.append(jnp.sum(taken[0:16], axis=0, keepdims=True))
        for g in range(2, 9):
            rows.append(jnp.sum(taken[8 * g:8 * g + 8], axis=0, keepdims=True))
        row8 = lax.broadcasted_iota(jnp.int32, (8, t), 0)
        n_lo = jnp.zeros((8, t), F32)
        for k in range(8):
            n_lo = jnp.where(row8 == k, rows[k], n_lo)
        n16 = jnp.concatenate([n_lo, taken[72:80]], axis=0)

        ex1 = jnp.exp(v1 - v1[0:1, :])
        ex2 = jnp.exp(v2 - v2[0:1, :])
        wts = jnp.concatenate(_pair_groups(ex1, ex2, lambda a, b: a * b, 0.0), axis=0)
        z = jnp.sum(taken * wts, axis=0, keepdims=True)
        inv_z = 1.0 / z

        n_dense = jnp.zeros((PEER_KEYS, t), F32)
        for r in range(PEER_TOPK):
            n_dense = jnp.where(rank1 == float(r), n16[r:r + 1, :], n_dense)
        e1_ref[h] = jnp.where(rank1 < float(PEER_TOPK), jnp.exp(s1 - v1[0:1, :]) * inv_z, 0.0)
        n1_ref[h] = n_dense
        e2_ref[h] = jnp.exp(s2 - v2[0:1, :]).astype(BF16)
        r2_ref[h] = rank2.astype(BF16)
        return carry

    lax.fori_loop(0, PEER_HEADS, head_body, 0)


def _peer_topk(qt, subk, tt=256):
    nq, t = qt.shape
    tab = jax.ShapeDtypeStruct((PEER_HEADS, PEER_KEYS, t), F32)
    tab16 = jax.ShapeDtypeStruct((PEER_HEADS, PEER_KEYS, t), BF16)
    tab_spec = pl.BlockSpec((PEER_HEADS, PEER_KEYS, tt), lambda i: (0, 0, i))
    return pl.pallas_call(
        _peer_topk_kernel,
        grid=(t // tt,),
        in_specs=[pl.BlockSpec((nq, tt), lambda i: (0, i)),
                  pl.BlockSpec(subk.shape, lambda i: (0, 0, 0))],
        out_specs=[tab_spec] * 4,
        out_shape=[tab, tab, tab16, tab16],
        compiler_params=pltpu.CompilerParams(dimension_semantics=("parallel",),
                                             vmem_limit_bytes=VMEM_LIMIT),
        name="peer_topk",
    )(qt, subk)


def _peer_main_kernel(h2_ref, u_ref, vt_ref, e1_ref, n1_ref, e2_ref, r2_ref, x1_ref, mod_ref,
                      o_ref, acc_ref, wt_ref):
    e = pl.program_id(1)
    te = u_ref.shape[0]
    per = te // PEER_KEYS

    @pl.when(e == 0)
    def _():
        acc_ref[...] = jnp.zeros_like(acc_ref)

    at = jnp.dot(u_ref[...], h2_ref[...], preferred_element_type=F32)
    rows = pl.ds(pl.multiple_of(e * per, per), per)
    for lt in range(at.shape[1] // LANES):
        lsl = slice(lt * LANES, (lt + 1) * LANES)
        n8 = [n1_ref[h, rows, lsl] for h in range(PEER_HEADS)]
        e8 = [e1_ref[h, rows, lsl] for h in range(PEER_HEADS)]
        for ii in range(per):
            g = jnp.zeros((PEER_KEYS, LANES), BF16)
            for h in range(PEER_HEADS):
                nrow = n8[h][ii:ii + 1, :].astype(BF16)
                erow = e8[h][ii:ii + 1, :].astype(BF16)
                g = g + jnp.where(r2_ref[h, :, lsl] < nrow, e2_ref[h, :, lsl], jnp.zeros((), BF16)) * erow
            a = at[ii * PEER_KEYS:(ii + 1) * PEER_KEYS, lsl]
            wt_ref[ii * PEER_KEYS:(ii + 1) * PEER_KEYS, lsl] = _gelu_tanh(a).astype(BF16) * g
    acc_ref[...] += jnp.dot(vt_ref[...], wt_ref[...], preferred_element_type=F32)

    @pl.when(e == pl.num_programs(1) - 1)
    def _():
        o_ref[...] = x1_ref[...] + mod_ref[0, 5:6, :] * acc_ref[...].T


def _peer_main(h2, u, vt, e1, n1, e2, r2, x1, mod, s, tt=512, te=1024):
    d, t = h2.shape
    ne = u.shape[0]
    per = s // tt
    tab_spec = pl.BlockSpec((PEER_HEADS, PEER_KEYS, tt), lambda i, e: (0, 0, i))
    return pl.pallas_call(
        _peer_main_kernel,
        grid=(t // tt, ne // te),
        in_specs=[pl.BlockSpec((d, tt), lambda i, e: (0, i)),
                  pl.BlockSpec((te, d), lambda i, e: (e, 0)),
                  pl.BlockSpec((d, te), lambda i, e: (0, e)),
                  tab_spec, tab_spec, tab_spec, tab_spec,
                  pl.BlockSpec((tt, d), lambda i, e: (i, 0)),
                  pl.BlockSpec((1, 6, d), lambda i, e: (i // per, 0, 0))],
        out_specs=pl.BlockSpec((tt, d), lambda i, e: (i, 0)),
        out_shape=jax.ShapeDtypeStruct((t, d), F32),
        scratch_shapes=[pltpu.VMEM((d, tt), F32), pltpu.VMEM((te, tt), BF16)],
        compiler_params=pltpu.CompilerParams(dimension_semantics=("parallel", "arbitrary"),
                                             vmem_limit_bytes=VMEM_LIMIT),
        name="peer_main",
    )(h2, u, vt, e1, n1, e2, r2, x1, mod)


def _layer(x, silu_in, w_ada, b_ada, norm1_g, w_in, b_fgate, fox_q_g, fox_k_g, moba_q_g, moba_k_g,
           w_up_fox, w_up_moba, w_out, norm2_g, w_peer_q, peer_sub_keys, peer_u, peer_v):
    bsz, s, d = x.shape
    w = BRANCH_WIDTH
    mod = _ada_mod(silu_in, w_ada, b_ada).reshape(bsz, 6, d)

    col = np.cumsum([0, w, w, w, N_HEADS, w, w, w, 2 * w, 2 * w])
    grp = lambda j: w_in[:, col[j]:col[j + 1]]
    wmain = jnp.concatenate([grp(0), grp(1), grp(4), grp(5), grp(7), grp(8)], axis=1).astype(BF16)
    wvt = jnp.stack([grp(2).T, grp(6).T]).astype(BF16)
    wf = jnp.pad(jnp.tile(grp(3), (1, 3)), ((0, 0), (0, LANES - FOX_ONE_ROW))).astype(BF16)
    bf = jnp.pad(jnp.tile(b_fgate, 3), (0, LANES - FOX_ONE_ROW)).reshape(1, LANES)
    gains = jnp.stack([jnp.tile(g, N_HEADS) for g in (fox_q_g, fox_k_g, moba_q_g, moba_k_g)])
    hid = jnp.arange(w) // HEAD_DIM
    bd = (hid[:, None] == hid[None, :]).astype(BF16)
    efox, emq, emk = (jnp.asarray(m, BF16) for m in _placement_matrices())

    qxa, kxa, vta, qxb, kxb, vtb, sga, sgb = _in_proj(
        x, mod, norm1_g.reshape(1, d), wmain, wvt, wf, bf, gains, bd, efox, emq, emk)
    ya = _attention("attn_fox", qxa, kxa, vta, bsz, s)
    yb = _attention("attn_moba", qxb, kxb, vtb, bsz, s)

    x1, h2, qt = _merge(ya, yb, sga, sgb, x.reshape(bsz * s, d), mod,
                        w_up_fox.astype(BF16), w_up_moba.astype(BF16), w_out.astype(BF16),
                        norm2_g.reshape(1, d), w_peer_q.T.astype(BF16), s)
    subk = peer_sub_keys.reshape(2 * PEER_HEADS, PEER_KEYS, PEER_HALF)
    e1, n1, e2, r2 = _peer_topk(qt, subk)
    out = _peer_main(h2, peer_u.astype(BF16), peer_v.T.astype(BF16), e1, n1, e2, r2, x1, mod, s)
    return out.reshape(bsz, s, d)


def kernel(x, c, w_ada, b_ada, norm1_g, w_in, b_fgate, fox_q_g, fox_k_g, moba_q_g, moba_k_g,
           w_up_fox, w_up_moba, w_out, norm2_g, w_peer_q, peer_sub_keys, peer_u, peer_v):
    depth = w_ada.shape[0]
    for l in range(depth):
        x = _layer(x, c, w_ada[l], b_ada[l], norm1_g[l], w_in[l], b_fgate[l], fox_q_g[l], fox_k_g[l],
                   moba_q_g[l], moba_k_g[l], w_up_fox[l], w_up_moba[l], w_out[l], norm2_g[l],
                   w_peer_q[l], peer_sub_keys[l], peer_u[l], peer_v[l])
    return x
```
